```python
import jax
import jax.numpy as jnp
from jax import lax
import numpy as np

D_MODEL = 4096
BATCH = 2
SEQ = 8192
DEPTH = 2

GRID_W = 64
CTX_LEN = 256
N_EVEN = (DEPTH + 1) // 2
N_ODD = DEPTH // 2
N_MOD = 6
W_A = D_MODEL // 2
W_B = D_MODEL // 2
A_CONV = 31
B_CONV = 3
AB_IN = 2 * W_A + 3 * W_B
M_HEADS = 8
M_DQK = D_MODEL // (2 * M_HEADS)
M_DV = D_MODEL // M_HEADS
M_CHUNK = 128
N_DIR = 2
Q_COLS = M_HEADS * M_DQK
V_COLS = M_HEADS * M_DV
G_COLS = N_DIR * 2 * M_HEADS
C_IN = 2 * Q_COLS + 2 * V_COLS + G_COLS
D_FF = 11008
FFN_CONV = 3
EPS = 1e-6

kernel_name = 'hybrid_conformer_shortconv_mlstm_dit'


def rms_norm(x, g):
    xf = x.astype(jnp.float32)
    y = xf * lax.rsqrt(jnp.mean(xf * xf, axis=-1, keepdims=True) + EPS)
    return (y * g).astype(x.dtype)


def layer_norm(x, g, b):
    xf = x.astype(jnp.float32)
    mu = jnp.mean(xf, axis=-1, keepdims=True)
    xc = xf - mu
    y = xc * lax.rsqrt(jnp.mean(xc * xc, axis=-1, keepdims=True) + EPS)
    return (y * g + b).astype(x.dtype)


def modulate(h, shift, scale):
    return h * (1 + scale[:, None, :]) + shift[:, None, :]


def dwconv2d(x4, w):
    return lax.conv_general_dilated(x4, w[:, :, None, :], (1, 1), 'SAME',
                                    dimension_numbers=('NHWC', 'HWIO', 'NHWC'),
                                    feature_group_count=x4.shape[-1])


def conv_grid(x, w, rows, vertical):
    b, s, ch = x.shape
    k = w[:, None, :] if vertical else w[None, :, :]
    return dwconv2d(x.reshape(b, rows, GRID_W, ch), k).reshape(b, s, ch)


def conv_seq(x, w):
    return dwconv2d(x[:, None], w[None])[:, 0]


def mixer_ab(h, w_in, a_conv_w, a_conv_b, a_ln_g, a_ln_b, b_conv_w, w_out, rows):
    a_val, a_gate, b_b, b_c, b_x = jnp.split(
        h @ w_in, [W_A, 2 * W_A, 2 * W_A + W_B, 2 * W_A + 2 * W_B], axis=-1)
    u = a_val * jax.nn.sigmoid(a_gate)
    z = b_c * b_x
    if rows is None:
        u = conv_seq(u, a_conv_w)
        z = conv_seq(z, b_conv_w)
    else:
        u = conv_grid(u, a_conv_w, rows, False)
        z = conv_grid(z, b_conv_w, rows, True)
    u = jax.nn.silu(layer_norm(u + a_conv_b, a_ln_g, a_ln_b))
    return jnp.concatenate([u, b_b * z], axis=-1) @ w_out


def conv_ffn(h, w_up, conv_w, w_down, rows):
    u = h @ w_up
    u = conv_seq(u, conv_w) if rows is None else conv_grid(u, conv_w, rows, False)
    gate, val = jnp.split(u, 2, axis=-1)
    return (jax.nn.silu(gate) * val) @ w_down


def mlstm_project(h, w_in, b_gates):
    b, L, _ = h.shape
    q, k, v, o, g = jnp.split(
        h @ w_in, [Q_COLS, 2 * Q_COLS, 2 * Q_COLS + V_COLS, 2 * Q_COLS + 2 * V_COLS], axis=-1)
    heads = lambda t, d: jnp.transpose(t.reshape(b, L, M_HEADS, d), (0, 2, 1, 3)).astype(jnp.float32)
    q = heads(q, M_DQK)
    k = heads(k, M_DQK) * (M_DQK ** -0.5)
    v = heads(v, M_DV)
    g = (g + b_gates).astype(jnp.float32).reshape(b, L, N_DIR, 2, M_HEADS)
    ig = jnp.transpose(g[:, :, :, 0], (2, 0, 3, 1))
    lf = jax.nn.log_sigmoid(jnp.transpose(g[:, :, :, 1], (2, 0, 3, 1)))
    return q, k, v, o, ig, lf


def mlstm_scan(q, k, v, ig, lf, state):
    b, hh, L, _ = q.shape
    nc = L // M_CHUNK
    tril = jnp.tril(jnp.ones((M_CHUNK, M_CHUNK), bool))

    def to_chunks(t):
        return jnp.moveaxis(t.reshape(b, hh, nc, M_CHUNK, *t.shape[3:]), 2, 0)

    def step(carry, inp):
        C, n, m = carry
        qc, kc, vc, ic, fc = inp
        bcum = jnp.cumsum(fc, axis=-1)
        dmat = jnp.where(tril, bcum[..., :, None] - bcum[..., None, :] + ic[..., None, :], -jnp.inf)
        inter = bcum + m[..., None]
        m_t = jnp.maximum(inter, jnp.max(dmat, axis=-1))
        w_inter = jnp.exp(inter - m_t)
        s = jnp.einsum('bhtd,bhsd->bhts', qc, kc) * jnp.exp(dmat - m_t[..., None])
        num = jnp.einsum('bhts,bhsv->bhtv', s, vc) + w_inter[..., None] * jnp.einsum('bhvd,bhtd->bhtv', C, qc)
        den = jnp.sum(s, axis=-1) + w_inter * jnp.einsum('bhd,bhtd->bht', n, qc)
        h = num / jnp.maximum(jnp.abs(den), jnp.exp(-m_t))[..., None]
        g = bcum[..., -1]
        dend = g[..., None] - bcum + ic
        m_new = jnp.maximum(g + m, jnp.max(dend, axis=-1))
        w_old = jnp.exp(g + m - m_new)
        w_s = jnp.exp(dend - m_new[..., None])
        C_new = w_old[..., None, None] * C + jnp.einsum('bhsv,bhsd->bhvd', vc * w_s[..., None], kc)
        n_new = w_old[..., None] * n + jnp.einsum('bhs,bhsd->bhd', w_s, kc)
        return (C_new, n_new, m_new), h

    state, hs = lax.scan(step, state, tuple(map(to_chunks, (q, k, v, ig, lf))))
    return jnp.moveaxis(hs, 0, 2).reshape(b, hh, L, M_DV), state


def mlstm_out(hs, o, hn_g, w_out):
    b, hh, L, dv = hs.shape
    ht = jnp.transpose(hs, (0, 2, 1, 3))
    ht = ht * lax.rsqrt(jnp.mean(ht * ht, axis=-1, keepdims=True) + EPS)
    ht = (ht.reshape(b, L, hh * dv) * hn_g).astype(o.dtype) * jax.nn.sigmoid(o)
    return ht @ w_out


def mixer_c(h_lat, h_ctx, w_in, b_gates, hn_g, w_out, need_ctx):
    ql, kl, vl, ol, igl, lfl = mlstm_project(h_lat, w_in, b_gates)
    qc, kc, vc, oc, igc, lfc = mlstm_project(h_ctx, w_in, b_gates)
    b = h_lat.shape[0]
    zero = (jnp.zeros((b, M_HEADS, M_DV, M_DQK), jnp.float32),
            jnp.zeros((b, M_HEADS, M_DQK), jnp.float32),
            jnp.zeros((b, M_HEADS), jnp.float32))
    fl = lambda t: jnp.flip(t, axis=2)
    hc_f, st_f = mlstm_scan(qc, kc, vc, igc[0], lfc[0], zero)
    hl_f, _ = mlstm_scan(ql, kl, vl, igl[0], lfl[0], st_f)
    hc_b, st_b = mlstm_scan(fl(qc), fl(kc), fl(vc), fl(igc[1]), fl(lfc[1]), zero)
    hl_b, _ = mlstm_scan(fl(ql), fl(kl), fl(vl), fl(igl[1]), fl(lfl[1]), st_b)
    y_lat = mlstm_out(hl_f + fl(hl_b), ol, hn_g, w_out)
    y_ctx = mlstm_out(hc_f + fl(hc_b), oc, hn_g, w_out) if need_ctx else None
    return y_lat, y_ctx


def setup_inputs(seed: int = 0) -> dict:
    key = jax.random.key(seed)
    ks = iter(jax.random.split(key, 32))
    D = D_MODEL

    def nrm(shape, scale):
        return scale * jax.random.normal(next(ks), shape, jnp.float32)

    gates_i = nrm((N_ODD, N_DIR, M_HEADS), 0.1)
    gates_f = 3.0 + nrm((N_ODD, N_DIR, M_HEADS), 0.5)
    m_b_gates = jnp.stack([gates_i, gates_f], axis=2).reshape(N_ODD, G_COLS)
    return {
        'x': nrm((BATCH, SEQ, D), 1.0),
        'c': nrm((BATCH, D), 1.0),
        'ctx': nrm((BATCH, CTX_LEN, D), 1.0),
        'c_ctx': nrm((D,), 1.0),
        'ada_w': nrm((DEPTH, D, N_MOD * D), 0.5 * D ** -0.5),
        'ada_b': nrm((DEPTH, N_MOD * D), 0.02),
        'norm_g': 1.0 + nrm((DEPTH, 4, D), 0.1),
        'ab_w_in': nrm((N_EVEN, D, AB_IN), D ** -0.5),
        'a_conv_w': nrm((N_EVEN, A_CONV, W_A), A_CONV ** -0.5),
        'a_conv_b': nrm((N_EVEN, W_A), 0.02),
        'a_ln_g': 1.0 + nrm((N_EVEN, W_A), 0.1),
        'a_ln_b': nrm((N_EVEN, W_A), 0.02),
        'b_conv_w': nrm((N_EVEN, B_CONV, W_B), B_CONV ** -0.5),
        'ab_w_out': nrm((N_EVEN, W_A + W_B, D), (W_A + W_B) ** -0.5),
        'm_w_in': nrm((N_ODD, D, C_IN), D ** -0.5),
        'm_b_gates': m_b_gates,
        'm_hn_g': 1.0 + nrm((N_ODD, V_COLS), 0.1),
        'm_w_out': nrm((N_ODD, V_COLS, D), V_COLS ** -0.5),
        'ffn_w_up': nrm((DEPTH, D, 2 * D_FF), D ** -0.5),
        'ffn_conv_w': nrm((DEPTH, FFN_CONV, 2 * D_FF), FFN_CONV ** -0.5),
        'ffn_w_down': nrm((DEPTH, D_FF, D), D_FF ** -0.5),
    }


def reference(x, c, ctx, c_ctx, ada_w, ada_b, norm_g, ab_w_in, a_conv_w, a_conv_b, a_ln_g, a_ln_b,
              b_conv_w, ab_w_out, m_w_in, m_b_gates, m_hn_g, m_w_out, ffn_w_up, ffn_conv_w, ffn_w_down):
    rows = x.shape[1] // GRID_W
    for layer in range(DEPTH):
        last = layer == DEPTH - 1
        j = layer // 2
        mx = jnp.split(jax.nn.silu(c) @ ada_w[layer] + ada_b[layer], N_MOD, axis=-1)
        mc = jnp.split(jax.nn.silu(c_ctx)[None] @ ada_w[layer] + ada_b[layer], N_MOD, axis=-1)
        g_pre, g_post, f_pre, f_post = norm_g[layer][0], norm_g[layer][1], norm_g[layer][2], norm_g[layer][3]
        hx = modulate(rms_norm(x, g_pre), mx[0], mx[1])
        hc = modulate(rms_norm(ctx, g_pre), mc[0], mc[1])
        if layer % 2 == 0:
            yx = mixer_ab(hx, ab_w_in[j], a_conv_w[j], a_conv_b[j], a_ln_g[j], a_ln_b[j],
                          b_conv_w[j], ab_w_out[j], rows)
            yc = None if last else mixer_ab(hc, ab_w_in[j], a_conv_w[j], a_conv_b[j], a_ln_g[j],
                                            a_ln_b[j], b_conv_w[j], ab_w_out[j], None)
        else:
            yx, yc = mixer_c(hx, hc, m_w_in[j], m_b_gates[j], m_hn_g[j], m_w_out[j], not last)
        x = x + mx[2][:, None, :] * rms_norm(yx, g_post)
        fx = conv_ffn(modulate(rms_norm(x, f_pre), mx[3], mx[4]),
                      ffn_w_up[layer], ffn_conv_w[layer], ffn_w_down[layer], rows)
        x = x + mx[5][:, None, :] * rms_norm(fx, f_post)
        if not last:
            ctx = ctx + mc[2][:, None, :] * rms_norm(yc, g_post)
            fc = conv_ffn(modulate(rms_norm(ctx, f_pre), mc[3], mc[4]),
                          ffn_w_up[layer], ffn_conv_w[layer], ffn_w_down[layer], None)
            ctx = ctx + mc[5][:, None, :] * rms_norm(fc, f_post)
    return x
```

```python
import functools

import jax
import jax.numpy as jnp
from jax import lax
from jax.experimental import pallas as pl
from jax.experimental.pallas import tpu as pltpu

F32 = jnp.float32
BF16 = jnp.bfloat16

EPS = 1e-6
GRID_W = 64
N_MOD = 6
A_CONV = 31
M_HEADS = 8
M_CHUNK = 128
LANES = 128
A_PAD = 16
MIB = 1024 * 1024


def _cparams(semantics, vmem_mib):
    return pltpu.CompilerParams(dimension_semantics=semantics, vmem_limit_bytes=vmem_mib * MIB)


def _sigmoid(v):
    return 1.0 / (1.0 + jnp.exp(-v))


def _rms(v):
    return v * lax.rsqrt(jnp.mean(v * v, axis=-1, keepdims=True) + EPS)


def _ada_kernel(c_ref, w_ref, b_ref, o_ref):
    cv = c_ref[...]
    s = cv * _sigmoid(cv)
    o_ref[...] = jnp.dot(s.astype(BF16), w_ref[...].astype(BF16), preferred_element_type=F32) + b_ref[...]


def _ada(cc, ada_w, ada_b, tn=512):
    depth, d, n = ada_w.shape
    return pl.pallas_call(
        _ada_kernel,
        grid=(depth, n // tn),
        in_specs=[
            pl.BlockSpec((8, d), lambda l, j: (0, 0)),
            pl.BlockSpec((None, d, tn), lambda l, j: (l, 0, j)),
            pl.BlockSpec((None, 1, tn), lambda l, j: (l, 0, j)),
        ],
        out_specs=pl.BlockSpec((None, 8, tn), lambda l, j: (l, 0, j)),
        out_shape=jax.ShapeDtypeStruct((depth, 8, n), F32),
        compiler_params=_cparams(("arbitrary", "arbitrary"), 40),
        name="ada_mod",
    )(cc, ada_w, ada_b.reshape(depth, 1, n))


def _row_kernel(*refs, has_y, has_h):
    it = iter(refs)
    x_ref = next(it)
    if has_y:
        y_ref, gate_ref, gpost_ref = next(it), next(it), next(it)
    if has_h:
        gnext_ref, shift_ref, scale_ref = next(it), next(it), next(it)
    x1 = x_ref[0]
    if has_y:
        x1 = x1 + gate_ref[0] * (_rms(y_ref[0].astype(F32)) * gpost_ref[...])
        next(it)[0] = x1
    if has_h:
        hn = _rms(x1) * gnext_ref[...]
        next(it)[0] = (hn * (1.0 + scale_ref[0]) + shift_ref[0]).astype(BF16)


def _rows(x, y=None, gate=None, g_post=None, g_next=None, shift=None, scale=None, tr=256):
    b, l, d = x.shape
    has_y, has_h = y is not None, g_next is not None
    tile = pl.BlockSpec((1, tr, d), lambda i, j: (i, j, 0))
    vec = pl.BlockSpec((1, d), lambda i, j: (0, 0))

    def mod_spec(m):
        return pl.BlockSpec((1, 1, d), (lambda i, j: (i, 0, 0)) if m.shape[0] == b else (lambda i, j: (0, 0, 0)))

    args, specs = [x], [tile]
    if has_y:
        args += [y, gate, g_post.reshape(1, d)]
        specs += [tile, mod_spec(gate), vec]
    if has_h:
        args += [g_next.reshape(1, d), shift, scale]
        specs += [vec, mod_spec(shift), mod_spec(scale)]
    out_shape, out_specs = [], []
    if has_y:
        out_shape.append(jax.ShapeDtypeStruct((b, l, d), F32))
        out_specs.append(tile)
    if has_h:
        out_shape.append(jax.ShapeDtypeStruct((b, l, d), BF16))
        out_specs.append(tile)
    return pl.pallas_call(
        functools.partial(_row_kernel, has_y=has_y, has_h=has_h),
        grid=(b, l // tr),
        in_specs=specs,
        out_specs=out_specs,
        out_shape=out_shape,
        compiler_params=_cparams(("arbitrary", "arbitrary"), 48),
        name="row_norm",
    )(*args)


def _conv3_rows(v, w, row_len):
    tm = v.shape[0]
    t = lax.broadcasted_iota(jnp.int32, v.shape, 0) & (row_len - 1)
    prev = jnp.where(t == 0, 0.0, pltpu.roll(v, 1, 0))
    nxt = jnp.where(t == row_len - 1, 0.0, pltpu.roll(v, tm - 1, 0))
    return w[0:1] * prev + w[1:2] * v + w[2:3] * nxt


def _mm_kernel(*refs, epi, row_len):
    if epi == "conv_glu":
        a_ref, w_ref, cw_ref, o_ref = refs
    else:
        a_ref, w_ref, o_ref = refs
    r = jnp.dot(a_ref[...], w_ref[...], preferred_element_type=F32)
    if epi == "none":
        out = r
    else:
        half = r.shape[1] // 2
        p, q = r[:, :half], r[:, half:]
        if epi == "glu":
            out = p * _sigmoid(q)
        elif epi == "mul":
            out = p * q
        else:
            cw = cw_ref[...]
            p = _conv3_rows(p, cw[:, :half], row_len)
            q = _conv3_rows(q, cw[:, half:], row_len)
            out = (p * _sigmoid(p)) * q
    o_ref[...] = out.astype(o_ref.dtype)


def _mm(a, w, *, tm, tn, epi="none", conv_w=None, row_len=None, out_dtype=BF16, vmem_mib=48):
    m, k = a.shape
    n = w.shape[1]
    tn_out = tn if epi == "none" else tn // 2
    n_out = n if epi == "none" else n // 2
    args = [a, w]
    specs = [pl.BlockSpec((tm, k), lambda i, j: (i, 0)), pl.BlockSpec((k, tn), lambda i, j: (0, j))]
    if epi == "conv_glu":
        args.append(conv_w)
        specs.append(pl.BlockSpec((conv_w.shape[0], tn), lambda i, j: (0, j)))
    return pl.pallas_call(
        functools.partial(_mm_kernel, epi=epi, row_len=row_len),
        grid=(m // tm, n // tn),
        in_specs=specs,
        out_specs=pl.BlockSpec((tm, tn_out), lambda i, j: (i, j)),
        out_shape=jax.ShapeDtypeStruct((m, n_out), out_dtype),
        compiler_params=_cparams(("arbitrary", "arbitrary"), vmem_mib),
        name="mm_" + epi,
    )(*args)


def _pair_cols(wa, wb, blk):
    k, n = wa.shape
    return jnp.stack([wa.reshape(k, n // blk, blk), wb.reshape(k, n // blk, blk)], axis=2).reshape(k, 2 * n)


def _mixpost_kernel(*refs, row_len, vertical, n_tiles):
    if vertical:
        (u_ref, z_ref, zp_ref, zn_ref, bb_ref, aw_ref, ab_ref, lg_ref, lb_ref, bw_ref,
         o_ref, upad_ref, conv_ref) = refs
    else:
        u_ref, z_ref, bb_ref, aw_ref, ab_ref, lg_ref, lb_ref, bw_ref, o_ref, upad_ref, conv_ref = refs
    tt, c = u_ref.shape[1], u_ref.shape[2]
    ncb = c // LANES
    n_rows = tt // row_len

    zpad = jnp.zeros((A_PAD, LANES), F32)
    for cb in range(ncb):
        for r in range(n_rows):
            upad_ref[cb, r, 0:A_PAD, :] = zpad
            upad_ref[cb, r, A_PAD + row_len:2 * A_PAD + row_len, :] = zpad
            upad_ref[cb, r, A_PAD:A_PAD + row_len, :] = (
                u_ref[0, r * row_len:(r + 1) * row_len, cb * LANES:(cb + 1) * LANES].astype(F32))

    def conv_body(idx, carry):
        cb = idx // n_rows
        r = idx - cb * n_rows
        acc = jnp.zeros((row_len, LANES), F32)
        for k in range(A_CONV):
            lo = A_PAD - A_CONV // 2 + k
            acc = acc + aw_ref[cb, k:k + 1, :] * upad_ref[cb, r, pl.ds(lo, row_len), :]
        conv_ref[cb, pl.ds(pl.multiple_of(r * row_len, row_len), row_len), :] = acc + ab_ref[cb]
        return carry

    lax.fori_loop(0, ncb * n_rows, conv_body, 0)

    tot = conv_ref[0]
    for cb in range(1, ncb):
        tot = tot + conv_ref[cb]
    mu = jnp.sum(tot, axis=-1, keepdims=True) * (1.0 / c)
    sq = jnp.zeros((tt, LANES), F32)
    for cb in range(ncb):
        dv = conv_ref[cb] - mu
        sq = sq + dv * dv
    rstd = lax.rsqrt(jnp.sum(sq, axis=-1, keepdims=True) * (1.0 / c) + EPS)
    for cb in range(ncb):
        y = (conv_ref[cb] - mu) * rstd * lg_ref[cb] + lb_ref[cb]
        o_ref[0, :, cb * LANES:(cb + 1) * LANES] = (y * _sigmoid(y)).astype(o_ref.dtype)

    i = pl.program_id(1)
    cz = 4 * LANES
    for s in range(c // cz):
        sl = slice(s * cz, (s + 1) * cz)
        zc = z_ref[0, :, sl].astype(F32)
        if vertical:
            top = jnp.where(i == 0, 0.0, zp_ref[0, :, sl].astype(F32))
            bot = jnp.where(i == n_tiles - 1, 0.0, zn_ref[0, :, sl].astype(F32))
            if n_rows > 1:
                up = jnp.concatenate([top, zc[:tt - row_len]], axis=0)
                dn = jnp.concatenate([zc[row_len:], bot], axis=0)
            else:
                up, dn = top, bot
        else:
            t = lax.broadcasted_iota(jnp.int32, zc.shape, 0)
            up = jnp.where(t == 0, 0.0, pltpu.roll(zc, 1, 0))
            dn = jnp.where(t == tt - 1, 0.0, pltpu.roll(zc, tt - 1, 0))
        bw = bw_ref[:, sl]
        zz = bw[0:1] * up + bw[1:2] * zc + bw[2:3] * dn
        o_ref[0, :, c + s * cz:c + (s + 1) * cz] = (bb_ref[0, :, sl].astype(F32) * zz).astype(o_ref.dtype)


def _mixpost(u, z, bb, a_conv_w, a_conv_b, a_ln_g, a_ln_b, b_conv_w, *, row_len, vertical, tt):
    b, l, c = u.shape
    ncb = c // LANES
    n_tiles = l // tt
    rpt = tt // row_len
    tile = pl.BlockSpec((1, tt, c), lambda i, j: (i, j, 0))

    def chan(v):
        return jnp.transpose(v.reshape(v.shape[0], ncb, LANES), (1, 0, 2))

    def cspec(rows):
        return pl.BlockSpec((ncb, rows, LANES), lambda i, j: (0, 0, 0))

    args, specs = [u, z], [tile, tile]
    if vertical:
        n_grid_rows = l // row_len
        args += [z, z]
        specs += [
            pl.BlockSpec((1, row_len, c), lambda i, j: (i, jnp.maximum(j * rpt - 1, 0), 0)),
            pl.BlockSpec((1, row_len, c), lambda i, j: (i, jnp.minimum((j + 1) * rpt, n_grid_rows - 1), 0)),
        ]
    args += [bb, chan(a_conv_w), chan(a_conv_b[None]), chan(a_ln_g[None]), chan(a_ln_b[None]), b_conv_w]
    specs += [tile, cspec(A_CONV), cspec(1), cspec(1), cspec(1),
              pl.BlockSpec((b_conv_w.shape[0], c), lambda i, j: (0, 0))]
    return pl.pallas_call(
        functools.partial(_mixpost_kernel, row_len=row_len, vertical=vertical, n_tiles=n_tiles),
        grid=(b, n_tiles),
        in_specs=specs,
        out_specs=pl.BlockSpec((1, tt, 2 * c), lambda i, j: (i, j, 0)),
        out_shape=jax.ShapeDtypeStruct((b, l, 2 * c), BF16),
        scratch_shapes=[
            pltpu.VMEM((ncb, rpt, row_len + 2 * A_PAD, LANES), F32),
            pltpu.VMEM((ncb, tt, LANES), F32),
        ],
        compiler_params=_cparams(("arbitrary", "arbitrary"), 48),
        name="mix_post",
    )(*args)


def _log_sigmoid(v):
    return jnp.minimum(v, 0.0) - jnp.log(1.0 + jnp.exp(-jnp.abs(v)))


def _scan_kernel(qf_ref, kf_ref, vf_ref, qb_ref, kb_ref, vb_ref, gcf_ref, grf_ref, gcb_ref, grb_ref,
                 s0_ref, m0_ref, hf_ref, hb_ref, sout_ref, mout_ref, s_ref, m_ref, *, nc, dqk, dv):
    i = pl.program_id(2)
    t_len = qf_ref.shape[1]

    @pl.when(i == 0)
    def _():
        s_ref[...] = s0_ref[0, 0]
        m_ref[...] = m0_ref[0, 0]

    row = lax.broadcasted_iota(jnp.int32, (t_len, t_len), 0)
    col = lax.broadcasted_iota(jnp.int32, (t_len, t_len), 1)
    lower = col <= row
    upper = col >= row
    ones_col = (lax.broadcasted_iota(jnp.int32, (t_len, LANES), 1) == 0).astype(BF16)

    dirs = ((qf_ref, kf_ref, vf_ref, gcf_ref, grf_ref, hf_ref, lower, upper),
            (qb_ref, kb_ref, vb_ref, gcb_ref, grb_ref, hb_ref, upper, lower))
    for d, (q_ref, k_ref, v_ref, gc_ref, gr_ref, h_ref, mask, mask_t) in enumerate(dirs):
        q = q_ref[0]
        k = k_ref[0] * jnp.asarray(dqk ** -0.5, BF16)
        vaug = jnp.concatenate([v_ref[0], ones_col], axis=1)
        gcol = gc_ref[0, 0]
        grow = gr_ref[0, 0]
        ig_c = gcol[:, 2 * d:2 * d + 1]
        lf_c = _log_sigmoid(gcol[:, 2 * d + 1:2 * d + 2])
        ig_r = grow[2 * d:2 * d + 1, :]
        lf_r = _log_sigmoid(grow[2 * d + 1:2 * d + 2, :])
        bc = jnp.sum(jnp.where(mask, lf_r, 0.0), axis=1, keepdims=True)
        br = jnp.sum(jnp.where(mask_t, lf_c, 0.0), axis=0, keepdims=True)
        g = jnp.sum(lf_r, axis=1, keepdims=True)
        m_old = m_ref[d][0:1, 0:1]
        dmat = jnp.where(mask, bc - br + ig_r, -jnp.inf)
        inter = bc + m_old
        m_t = jnp.maximum(inter, jnp.max(dmat, axis=1, keepdims=True))
        w_inter = jnp.exp(inter - m_t)
        a = lax.dot_general(q, k, (((1,), (1,)), ((), ())), preferred_element_type=F32)
        p = (a * jnp.exp(dmat - m_t)).astype(BF16)
        s_old = s_ref[d]
        o = (jnp.dot(p, vaug, preferred_element_type=F32)
             + w_inter * jnp.dot(q, s_old.astype(BF16), preferred_element_type=F32))
        den = jnp.maximum(jnp.abs(o[:, dv:dv + 1]), jnp.exp(-m_t))
        h_ref[0] = (o[:, :dv] / den).astype(h_ref.dtype)
        dend_r = g - br + ig_r
        dend_c = g - bc + ig_c
        m_new = jnp.maximum(g + m_old, jnp.max(dend_r, axis=1, keepdims=True))
        w_old = jnp.exp(g + m_old - m_new)
        xs = (vaug.astype(F32) * jnp.exp(dend_c - m_new)).astype(BF16)
        s_ref[d] = w_old * s_old + lax.dot_general(k, xs, (((0,), (0,)), ((), ())), preferred_element_type=F32)
        m_ref[d] = jnp.broadcast_to(m_new, m_ref.shape[1:])

    @pl.when(i == nc - 1)
    def _():
        sout_ref[0, 0] = s_ref[...]
        mout_ref[0, 0] = m_ref[...]


def _mlstm_scan(qkvo, gates, s0, m0):
    b, l, _ = qkvo.shape
    hh, t = M_HEADS, M_CHUNK
    nc = l // t
    dv = (qkvo.shape[2] // 6 * 2) // hh
    dqk = dv // 2
    sw = dv + LANES
    g5 = gates.reshape(b, l, 2, 2, hh)
    gcol = jnp.transpose(g5, (0, 4, 1, 2, 3)).reshape(b, hh, l, 4)
    grow = jnp.transpose(g5, (0, 4, 2, 3, 1)).reshape(b, hh, 4, l)
    kq, kv = hh, (2 * hh * dqk) // dv

    def fwd(f):
        return lambda bi, h, i: f(bi, h, i)

    def bwd(f):
        return lambda bi, h, i: f(bi, h, nc - 1 - i)

    q_map = lambda bi, h, i: (bi, i, h)
    k_map = lambda bi, h, i: (bi, i, kq + h)
    v_map = lambda bi, h, i: (bi, i, kv + h)
    gc_map = lambda bi, h, i: (bi, h, i, 0)
    gr_map = lambda bi, h, i: (bi, h, 0, i)
    st_map = lambda bi, h, i: (bi, h, 0, 0, 0)
    specs = []
    for wrap in (fwd, bwd):
        specs += [pl.BlockSpec((1, t, dqk), wrap(q_map)), pl.BlockSpec((1, t, dqk), wrap(k_map)),
                  pl.BlockSpec((1, t, dv), wrap(v_map))]
    specs += [pl.BlockSpec((1, 1, t, 4), fwd(gc_map)), pl.BlockSpec((1, 1, 4, t), fwd(gr_map)),
              pl.BlockSpec((1, 1, t, 4), bwd(gc_map)), pl.BlockSpec((1, 1, 4, t), bwd(gr_map)),
              pl.BlockSpec((1, 1, 2, dqk, sw), st_map), pl.BlockSpec((1, 1, 2, 8, LANES), st_map)]
    h_shape = jax.ShapeDtypeStruct((b, l, hh * dv), BF16)
    return pl.pallas_call(
        functools.partial(_scan_kernel, nc=nc, dqk=dqk, dv=dv),
        grid=(b, hh, nc),
        in_specs=specs,
        out_specs=[pl.BlockSpec((1, t, dv), fwd(q_map)), pl.BlockSpec((1, t, dv), bwd(q_map)),
                   pl.BlockSpec((1, 1, 2, dqk, sw), st_map), pl.BlockSpec((1, 1, 2, 8, LANES), st_map)],
        out_shape=[h_shape, h_shape,
                   jax.ShapeDtypeStruct((b, hh, 2, dqk, sw), F32),
                   jax.ShapeDtypeStruct((b, hh, 2, 8, LANES), F32)],
        scratch_shapes=[pltpu.VMEM((2, dqk, sw), F32), pltpu.VMEM((2, 8, LANES), F32)],
        compiler_params=_cparams(("arbitrary", "arbitrary", "arbitrary"), 32),
        name="mlstm_scan",
    )(qkvo, qkvo, qkvo, qkvo, qkvo, qkvo, gcol, grow, gcol, grow, s0, m0)


def _mout_kernel(hf_ref, hb_ref, o_ref, g_ref, out_ref, *, dv):
    for h in range(hf_ref.shape[2] // dv):
        sl = slice(h * dv, (h + 1) * dv)
        ht = _rms(hf_ref[0, :, sl].astype(F32) + hb_ref[0, :, sl].astype(F32))
        out_ref[0, :, sl] = ((ht * g_ref[:, sl]) * _sigmoid(o_ref[0, :, sl].astype(F32))).astype(out_ref.dtype)


def _mlstm_out_pre(hf, hb, qkvo, hn_g, tt=256):
    b, l, v = hf.shape
    dv = v // M_HEADS
    o_blk = (qkvo.shape[2] - v) // v
    tile = pl.BlockSpec((1, tt, v), lambda i, j: (i, j, 0))
    return pl.pallas_call(
        functools.partial(_mout_kernel, dv=dv),
        grid=(b, l // tt),
        in_specs=[tile, tile, pl.BlockSpec((1, tt, v), lambda i, j: (i, j, o_blk)),
                  pl.BlockSpec((1, v), lambda i, j: (0, 0))],
        out_specs=tile,
        out_shape=jax.ShapeDtypeStruct((b, l, v), BF16),
        compiler_params=_cparams(("arbitrary", "arbitrary"), 32),
        name="mlstm_out_pre",
    )(hf, hb, qkvo, hn_g.reshape(1, v))


def _conv_ffn(h, w_up_p, conv_w_p, w_down, row_len, tm_up, tm_down):
    b, l, d = h.shape
    act = _mm(h.reshape(b * l, d), w_up_p, tm=tm_up, tn=512, epi="conv_glu", conv_w=conv_w_p, row_len=row_len)
    return _mm(act, w_down, tm=tm_down, tn=256).reshape(b, l, d)


def _mixer_ab(h, w_pairs, wb, w_out, conv_args, row_len, vertical, tm, tt):
    b, l, d = h.shape
    hf = h.reshape(b * l, d)
    wa, wz = w_pairs
    c = wb.shape[1]
    u = _mm(hf, wa, tm=tm, tn=512, epi="glu").reshape(b, l, c)
    z = _mm(hf, wz, tm=tm, tn=512, epi="mul").reshape(b, l, c)
    bb = _mm(hf, wb, tm=tm, tn=512).reshape(b, l, c)
    act = _mixpost(u, z, bb, *conv_args, row_len=row_len, vertical=vertical, tt=tt)
    return _mm(act.reshape(b * l, 2 * c), w_out, tm=tm, tn=512).reshape(b, l, d)


def kernel(x, c, ctx, c_ctx, ada_w, ada_b, norm_g, ab_w_in, a_conv_w, a_conv_b, a_ln_g, a_ln_b, b_conv_w,
           ab_w_out, m_w_in, m_b_gates, m_hn_g, m_w_out, ffn_w_up, ffn_conv_w, ffn_w_down):
    b, l, d = x.shape
    lc = ctx.shape[1]
    depth = ada_w.shape[0]
    assert depth == 2, "layer 0 = conv mixers, layer 1 = mLSTM (last)"
    d_ff = ffn_w_down.shape[1]
    w_a = ab_w_out.shape[1] // 2
    q_cols = M_HEADS * (d // (2 * M_HEADS))
    v_cols = d

    cc = jnp.zeros((8, d), F32).at[:b].set(c).at[b].set(c_ctx)
    mods = _ada(cc, ada_w, ada_b)

    def mod(layer, kk, lat):
        rows = mods[layer, :b] if lat else mods[layer, b:b + 1]
        return rows[:, None, kk * d:(kk + 1) * d]

    def ffn_weights(layer):
        w = ffn_w_up[layer].astype(BF16)
        cw = ffn_conv_w[layer]
        return (_pair_cols(w[:, :d_ff], w[:, d_ff:], 256), _pair_cols(cw[:, :d_ff], cw[:, d_ff:], 256),
                ffn_w_down[layer].astype(BF16))

    ng = norm_g[0]
    w_in = ab_w_in[0].astype(BF16)
    w_pairs = (_pair_cols(w_in[:, :w_a], w_in[:, w_a:2 * w_a], 256),
               _pair_cols(w_in[:, 3 * w_a:4 * w_a], w_in[:, 4 * w_a:], 256))
    wb = w_in[:, 2 * w_a:3 * w_a]
    w_out = ab_w_out[0].astype(BF16)
    conv_args = (a_conv_w[0], a_conv_b[0], a_ln_g[0], a_ln_b[0], b_conv_w[0])
    w_up_p, cw_p, w_down = ffn_weights(0)

    def layer0(v, lat):
        row_len = GRID_W if lat else lc
        tm = 1024 if lat else lc
        (h,) = _rows(v, g_next=ng[0], shift=mod(0, 0, lat), scale=mod(0, 1, lat))
        y = _mixer_ab(h, w_pairs, wb, w_out, conv_args, row_len, lat, tm, 256)
        v, h = _rows(v, y, mod(0, 2, lat), ng[1], ng[2], mod(0, 3, lat), mod(0, 4, lat))
        f = _conv_ffn(h, w_up_p, cw_p, w_down, row_len, tm, 512 if lat else lc)
        return v, f

    x, fx = layer0(x, True)
    ctx, fc = layer0(ctx, False)

    ng1 = norm_g[1]
    x, hx = _rows(x, fx, mod(0, 5, True), ng[3], ng1[0], mod(1, 0, True), mod(1, 1, True))
    ctx, hc = _rows(ctx, fc, mod(0, 5, False), ng[3], ng1[0], mod(1, 0, False), mod(1, 1, False))
    n_main = 2 * q_cols + 2 * v_cols
    w_m = m_w_in[0][:, :n_main].astype(BF16)
    w_g = jnp.pad(m_w_in[0][:, n_main:], ((0, 0), (0, LANES - (m_w_in.shape[2] - n_main)))).astype(BF16)
    n_gate = m_b_gates.shape[1]

    def project(h, tm):
        bb_, ll, _ = h.shape
        hf = h.reshape(bb_ * ll, d)
        qkvo = _mm(hf, w_m, tm=tm, tn=512).reshape(bb_, ll, n_main)
        g = _mm(hf, w_g, tm=tm, tn=LANES, out_dtype=F32)[:, :n_gate] + m_b_gates[0]
        return qkvo, g.reshape(bb_, ll, n_gate)

    qkvo_c, g_c = project(hc, lc)
    qkvo_l, g_l = project(hx, 1024)
    dqk = q_cols // M_HEADS
    s0 = jnp.zeros((b, M_HEADS, 2, dqk, v_cols // M_HEADS + LANES), F32)
    m0 = jnp.zeros((b, M_HEADS, 2, 8, LANES), F32)
    _, _, s_c, m_c = _mlstm_scan(qkvo_c, g_c, s0, m0)
    hf, hb, _, _ = _mlstm_scan(qkvo_l, g_l, s_c, m_c)
    act = _mlstm_out_pre(hf, hb, qkvo_l, m_hn_g[0])
    y = _mm(act.reshape(b * l, v_cols), m_w_out[0].astype(BF16), tm=1024, tn=512).reshape(b, l, d)
    x, h = _rows(x, y, mod(1, 2, True), ng1[1], ng1[2], mod(1, 3, True), mod(1, 4, True))
    w_up_p, cw_p, w_down = ffn_weights(1)
    fx = _conv_ffn(h, w_up_p, cw_p, w_down, GRID_W, 1024, 512)
    (x,) = _rows(x, fx, mod(1, 5, True), ng1[3])
    return x
```
